```python
import math
import jax, jax.numpy as jnp
from jax import lax
import numpy as np

D_MODEL = 2048
BATCH = 2
SEQ = 8192
DEPTH = 1

N_META = 16
POOL_WIDTH = 1024
POOL_GROUPS = 4
POOL_WINDOWS = (2, 4, 8, 16)
POOL_GC = POOL_WIDTH // POOL_GROUPS
CONV_WIDTH = 1024
CONV_K = 3
N_BRANCH = 2
IN_COLS = POOL_WIDTH + 3 * CONV_WIDTH + N_BRANCH * D_MODEL
PEER_HEADS = 8
PEER_QDIM = 256
PEER_HALF = PEER_QDIM // 2
N_KEYS = 128
N_EXPERTS = N_KEYS * N_KEYS
PEER_TOPK = 16
PEER_BLOCK = 128
DN_ALPHA = (2.0 * DEPTH) ** 0.25
DN_BETA = (8.0 * DEPTH) ** -0.25
LN_EPS = 1e-5

kernel_name = "hybrid_pool_shortconv_peer_deepnorm_block"


def _layer_norm(x, g, b):
    xf = x.astype(jnp.float32)
    mu = jnp.mean(xf, axis=-1, keepdims=True)
    var = jnp.mean(jnp.square(xf - mu), axis=-1, keepdims=True)
    y = (xf - mu) * lax.rsqrt(var + LN_EPS) * g.astype(jnp.float32) + b.astype(jnp.float32)
    return y.astype(x.dtype)


def _causal_multiscale_pool(a):
    bsz, t_len, _ = a.shape
    af = a.astype(jnp.float32).reshape(bsz, t_len, POOL_GROUPS, POOL_GC)
    cs = jnp.cumsum(af, axis=1)
    pos = jnp.arange(t_len)
    outs = []
    for g, w in enumerate(POOL_WINDOWS):
        csg = cs[:, :, g]
        lag = jnp.pad(csg, ((0, 0), (w, 0), (0, 0)))[:, :t_len]
        cnt = jnp.minimum(pos + 1, w).astype(jnp.float32)[None, :, None]
        outs.append((csg - lag) / cnt - af[:, :, g])
    return jnp.stack(outs, axis=2).astype(a.dtype)


def _causal_depthwise_conv(z, w, b):
    y = lax.conv_general_dilated(
        z, w, window_strides=(1,), padding=[(CONV_K - 1, 0)],
        dimension_numbers=("NWC", "WIO", "NWC"), feature_group_count=z.shape[-1])
    return y + b


def _peer(xf, w_q, keys_1, keys_2, expert_u, expert_v):
    n = xf.shape[0]
    n_pad = (-n) % PEER_BLOCK
    xb_all = jnp.pad(xf, ((0, n_pad), (0, 0))).reshape(-1, PEER_BLOCK, D_MODEL)

    def block(xb):
        q = (xb @ w_q).reshape(PEER_BLOCK, PEER_HEADS, 2, PEER_HALF)
        s1 = jnp.einsum("thc,hnc->thn", q[:, :, 0], keys_1)
        s2 = jnp.einsum("thc,hnc->thn", q[:, :, 1], keys_2)
        v1, i1 = lax.top_k(s1, PEER_TOPK)
        v2, i2 = lax.top_k(s2, PEER_TOPK)
        n_cand = PEER_TOPK * PEER_TOPK
        cand_s = (v1[..., :, None] + v2[..., None, :]).reshape(PEER_BLOCK, PEER_HEADS, n_cand)
        cand_i = (i1[..., :, None] * N_KEYS + i2[..., None, :]).reshape(PEER_BLOCK, PEER_HEADS, n_cand)
        top_s, sel = lax.top_k(cand_s, PEER_TOPK)
        idx = jnp.take_along_axis(cand_i, sel, axis=-1).reshape(PEER_BLOCK, PEER_HEADS * PEER_TOPK)
        gate = jax.nn.softmax(top_s.astype(jnp.float32), axis=-1).reshape(PEER_BLOCK, PEER_HEADS * PEER_TOPK)
        u = expert_u[idx]
        act = jax.nn.gelu(jnp.einsum("td,tkd->tk", xb, u).astype(jnp.float32), approximate=False)
        coef = (gate * act).astype(xb.dtype)
        return jnp.einsum("tk,tkd->td", coef, expert_v[idx])

    return lax.map(block, xb_all).reshape(-1, D_MODEL)[:n]


def setup_inputs(seed: int = 0) -> dict:
    key = jax.random.key(seed)
    ks = jax.random.split(key, 24)
    f32 = jnp.float32
    L = DEPTH

    def nrm(k, shape, scale):
        return jax.random.normal(k, shape, f32) * scale

    return {
        "x": nrm(ks[0], (BATCH, SEQ, D_MODEL), 1.0),
        "meta_tokens": nrm(ks[1], (N_META, D_MODEL), 1.0),
        "ln_in_g": 1.0 + nrm(ks[2], (D_MODEL,), 0.02),
        "ln_in_b": nrm(ks[3], (D_MODEL,), 0.02),
        "w_in": nrm(ks[4], (L, D_MODEL, IN_COLS), D_MODEL ** -0.5),
        "pool_w": nrm(ks[5], (L, POOL_GROUPS, POOL_GC, POOL_GC), POOL_GC ** -0.5),
        "pool_scale": 1.0 + nrm(ks[6], (L, POOL_WIDTH), 0.02),
        "pool_proj": nrm(ks[7], (L, POOL_WIDTH, D_MODEL), POOL_WIDTH ** -0.5),
        "conv_w": nrm(ks[8], (L, CONV_K, 1, CONV_WIDTH), CONV_K ** -0.5),
        "conv_b": nrm(ks[9], (L, CONV_WIDTH), 0.02),
        "conv_proj": nrm(ks[10], (L, CONV_WIDTH, D_MODEL), CONV_WIDTH ** -0.5),
        "w_out": nrm(ks[11], (L, D_MODEL, D_MODEL), DN_BETA * D_MODEL ** -0.5),
        "ln1_g": 1.0 + nrm(ks[12], (L, D_MODEL), 0.02),
        "ln1_b": nrm(ks[13], (L, D_MODEL), 0.02),
        "peer_wq": nrm(ks[14], (L, D_MODEL, PEER_HEADS * PEER_QDIM), D_MODEL ** -0.5),
        "peer_keys_1": nrm(ks[15], (L, PEER_HEADS, N_KEYS, PEER_HALF), PEER_HALF ** -0.5),
        "peer_keys_2": nrm(ks[16], (L, PEER_HEADS, N_KEYS, PEER_HALF), PEER_HALF ** -0.5),
        "expert_u": nrm(ks[17], (L, N_EXPERTS, D_MODEL), D_MODEL ** -0.5),
        "expert_v": nrm(ks[18], (L, N_EXPERTS, D_MODEL), DN_BETA * PEER_HEADS ** -0.5),
        "ln2_g": 1.0 + nrm(ks[19], (L, D_MODEL), 0.02),
        "ln2_b": nrm(ks[20], (L, D_MODEL), 0.02),
    }


def reference(x, meta_tokens, ln_in_g, ln_in_b, w_in, pool_w, pool_scale, pool_proj,
              conv_w, conv_b, conv_proj, w_out, ln1_g, ln1_b, peer_wq, peer_keys_1,
              peer_keys_2, expert_u, expert_v, ln2_g, ln2_b):
    bsz = x.shape[0]
    meta = jnp.broadcast_to(meta_tokens.astype(x.dtype)[None], (bsz, N_META, D_MODEL))
    h = jnp.concatenate([meta, x], axis=1)
    h = _layer_norm(h, ln_in_g, ln_in_b)
    t_len = h.shape[1]
    c1 = POOL_WIDTH
    c2 = c1 + CONV_WIDTH
    c3 = c2 + CONV_WIDTH
    c4 = c3 + CONV_WIDTH

    for l in range(DEPTH):
        proj = h @ w_in[l]
        a = proj[..., :c1]
        zh, zb, zc = proj[..., c1:c2], proj[..., c2:c3], proj[..., c3:c4]
        gates = jax.nn.sigmoid(proj[..., c4:].astype(jnp.float32)).astype(h.dtype)
        gates = gates.reshape(bsz, t_len, N_BRANCH, D_MODEL)

        pooled = _causal_multiscale_pool(a)
        ya = jnp.einsum("btgc,gcd->btgd", pooled, pool_w[l]).reshape(bsz, t_len, POOL_WIDTH)
        ya = (ya * pool_scale[l]) @ pool_proj[l]

        yb = zb * _causal_depthwise_conv(zc * zh, conv_w[l], conv_b[l])
        yb = yb @ conv_proj[l]

        mixed = (gates[:, :, 0] * ya + gates[:, :, 1] * yb) @ w_out[l]
        h = _layer_norm(DN_ALPHA * h + mixed, ln1_g[l], ln1_b[l])

        y = _peer(h.reshape(-1, D_MODEL), peer_wq[l], peer_keys_1[l], peer_keys_2[l],
                  expert_u[l], expert_v[l]).reshape(bsz, t_len, D_MODEL)
        h = _layer_norm(DN_ALPHA * h + y, ln2_g[l], ln2_b[l])

    return h[:, N_META:]
```

```python
import functools

import jax
import jax.numpy as jnp
from jax import lax
from jax.experimental import pallas as pl
from jax.experimental.pallas import tpu as pltpu

F32 = jnp.float32
BF16 = jnp.bfloat16

LN_EPS = 1e-5
N_META = 16
POOL_WINDOWS = (2, 4, 8, 16)
CONV_K = 3
N_KEYS = 128
TOPK = 16
RANK_NONE = float(TOPK)
LANES = 128
SUBLANES = 8
VMEM_LIMIT = 56 * 1024 * 1024

NT_DIMS = (((1,), (1,)), ((), ()))


def _layer_norm(x, g, b):
    mu = jnp.mean(x, axis=-1, keepdims=True)
    xc = x - mu
    var = jnp.mean(xc * xc, axis=-1, keepdims=True)
    return xc * lax.rsqrt(var + LN_EPS) * g + b


def _gelu_erf(x):
    return 0.5 * x * (1.0 + lax.erf(x * (0.5 ** 0.5)))


def _mixer_in_body(x_ref, meta_ref, g_ref, b_ref, w_ref, cw_ref, cb_ref,
                   pooled_ref, ybp_ref, gates_ref,
                   hn_s, zh_s, zb_s, atail_s, ptail_s, *, tb, blocks_per_seq, gc):
    i = pl.program_id(0)
    j = pl.program_id(1)
    seq_start = (i % blocks_per_seq) == 0

    @pl.when(j == 0)
    def _():
        hn_s[0:tb, :] = _layer_norm(x_ref[...], g_ref[...], b_ref[...]).astype(BF16)
        hn_s[tb:tb + N_META, :] = _layer_norm(meta_ref[...], g_ref[...], b_ref[...]).astype(BF16)

    chunk = jnp.dot(hn_s[...], w_ref[...], preferred_element_type=F32)

    def with_halo(cur, tail_s):
        @pl.when(seq_start)
        def _():
            tail_s[...] = cur[tb:tb + N_META]
        full = jnp.concatenate([cur[:tb], tail_s[...]], axis=0)
        tail_s[...] = cur[tb - N_META:tb]
        return full

    @pl.when(j == 0)
    def _():
        full = with_halo(chunk, atail_s)
        for g, w in enumerate(POOL_WINDOWS):
            s = full[:, g * gc:(g + 1) * gc]
            a_g = s[:tb]
            k = 1
            while k < w:
                s = s + pltpu.roll(s, k, axis=0)
                k *= 2
            pooled_ref[:, g * gc:(g + 1) * gc] = (s[:tb] * (1.0 / w) - a_g).astype(BF16)

    @pl.when(j == 1)
    def _():
        zh_s[...] = chunk

    @pl.when(j == 2)
    def _():
        zb_s[...] = chunk[:tb]

    @pl.when(j == 3)
    def _():
        full = with_halo(chunk * zh_s[...], ptail_s)
        conv = (cw_ref[2:3, :] * full + cw_ref[1:2, :] * pltpu.roll(full, 1, axis=0)
                + cw_ref[0:1, :] * pltpu.roll(full, 2, axis=0) + cb_ref[...])
        ybp_ref[...] = (zb_s[...] * conv[:tb]).astype(BF16)

    @pl.when(j >= 4)
    def _():
        gates_ref[...] = jax.nn.sigmoid(chunk[:tb]).astype(BF16)


def _mixer_in(x2, meta, g, b, w_in, conv_w, conv_b, *, tb, seq):
    n, d = x2.shape
    cw = conv_w.shape[1]
    n_chunks = w_in.shape[1] // cw
    gc = cw // len(POOL_WINDOWS)
    body = functools.partial(_mixer_in_body, tb=tb, blocks_per_seq=seq // tb, gc=gc)
    return pl.pallas_call(
        body,
        grid=(n // tb, n_chunks),
        in_specs=[
            pl.BlockSpec((tb, d), lambda i, j: (i, 0)),
            pl.BlockSpec((N_META, d), lambda i, j: (0, 0)),
            pl.BlockSpec((1, d), lambda i, j: (0, 0)),
            pl.BlockSpec((1, d), lambda i, j: (0, 0)),
            pl.BlockSpec((d, cw), lambda i, j: (0, j)),
            pl.BlockSpec((CONV_K, cw), lambda i, j: (0, 0)),
            pl.BlockSpec((1, cw), lambda i, j: (0, 0)),
        ],
        out_specs=[
            pl.BlockSpec((tb, cw), lambda i, j: (i, 0)),
            pl.BlockSpec((tb, cw), lambda i, j: (i, 0)),
            pl.BlockSpec((tb, cw), lambda i, j: (i, jnp.maximum(j - 4, 0))),
        ],
        out_shape=[
            jax.ShapeDtypeStruct((n, cw), BF16),
            jax.ShapeDtypeStruct((n, cw), BF16),
            jax.ShapeDtypeStruct((n, (n_chunks - 4) * cw), BF16),
        ],
        scratch_shapes=[
            pltpu.VMEM((tb + N_META, d), BF16),
            pltpu.VMEM((tb + N_META, cw), F32),
            pltpu.VMEM((tb, cw), F32),
            pltpu.VMEM((N_META, cw), F32),
            pltpu.VMEM((N_META, cw), F32),
        ],
        compiler_params=pltpu.CompilerParams(
            dimension_semantics=("arbitrary", "arbitrary"), vmem_limit_bytes=VMEM_LIMIT),
        name="mixer_in",
    )(x2, meta, g, b, w_in, conv_w, conv_b)


def _mixer_out_body(x_ref, gin_ref, bin_ref, pooled_ref, ybp_ref, gates_ref, poolw_ref, pscale_ref,
                    pproj_ref, cproj_ref, wout_ref, g1_ref, b1_ref, h1_ref, *, alpha, gc):
    d = x_ref.shape[1]
    hn = _layer_norm(x_ref[...], gin_ref[...], bin_ref[...])
    pooled = pooled_ref[...]
    parts = [jnp.dot(pooled[:, g * gc:(g + 1) * gc], poolw_ref[g], preferred_element_type=F32)
             for g in range(len(POOL_WINDOWS))]
    ya = (jnp.concatenate(parts, axis=1) * pscale_ref[...]).astype(BF16)
    ya = jnp.dot(ya, pproj_ref[...], preferred_element_type=F32)
    yb = jnp.dot(ybp_ref[...], cproj_ref[...], preferred_element_type=F32)
    merged = gates_ref[:, 0:d].astype(F32) * ya + gates_ref[:, d:2 * d].astype(F32) * yb
    mixed = jnp.dot(merged.astype(BF16), wout_ref[...], preferred_element_type=F32)
    h1_ref[...] = _layer_norm(alpha * hn + mixed, g1_ref[...], b1_ref[...])


def _mixer_out(x2, gin, bin_, pooled, ybp, gates, pool_w, pscale, pproj, cproj, wout, g1, b1,
               *, tb, alpha):
    n, d = x2.shape
    cw = pooled.shape[1]
    gc = cw // len(POOL_WINDOWS)
    const2 = lambda i: (0, 0)
    body = functools.partial(_mixer_out_body, alpha=alpha, gc=gc)
    return pl.pallas_call(
        body,
        grid=(n // tb,),
        in_specs=[
            pl.BlockSpec((tb, d), lambda i: (i, 0)),
            pl.BlockSpec((1, d), const2),
            pl.BlockSpec((1, d), const2),
            pl.BlockSpec((tb, cw), lambda i: (i, 0)),
            pl.BlockSpec((tb, cw), lambda i: (i, 0)),
            pl.BlockSpec((tb, 2 * d), lambda i: (i, 0)),
            pl.BlockSpec(pool_w.shape, lambda i: (0, 0, 0)),
            pl.BlockSpec((1, cw), const2),
            pl.BlockSpec((cw, d), const2),
            pl.BlockSpec((cw, d), const2),
            pl.BlockSpec((d, d), const2),
            pl.BlockSpec((1, d), const2),
            pl.BlockSpec((1, d), const2),
        ],
        out_specs=pl.BlockSpec((tb, d), lambda i: (i, 0)),
        out_shape=jax.ShapeDtypeStruct((n, d), F32),
        compiler_params=pltpu.CompilerParams(
            dimension_semantics=("arbitrary",), vmem_limit_bytes=VMEM_LIMIT),
        name="mixer_out",
    )(x2, gin, bin_, pooled, ybp, gates, pool_w, pscale, pproj, cproj, wout, g1, b1)


def _top_ranks(s):
    key = lax.broadcasted_iota(jnp.int32, s.shape, 0)
    rank = jnp.full(s.shape, RANK_NONE, F32)
    vals = []
    for a in range(TOPK):
        m = jnp.max(s, axis=0, keepdims=True)
        first = jnp.min(jnp.where(s == m, key, N_KEYS), axis=0, keepdims=True)
        hit = key == first
        rank = jnp.where(hit, float(a), rank)
        s = jnp.where(hit, -jnp.inf, s)
        vals.append(m)
    return jnp.concatenate(vals, axis=0), rank


def _pair_counts(v1, v2):
    sub = lax.broadcasted_iota(jnp.int32, (SUBLANES, LANES), 0)
    sub16 = lax.broadcasted_iota(jnp.int32, (TOPK, LANES), 0)
    neg = -jnp.inf
    cand, flat = [], []
    cand.append(v1[0:1] + v2)
    flat.append(sub16)
    for a in (1, 2, 3):
        cand.append(v1[a:a + 1] + v2[0:SUBLANES])
        flat.append(a * TOPK + sub)
    for b in (0, 1, 2):
        cand.append(jnp.where(sub >= 4, v1[0:SUBLANES] + v2[b:b + 1], neg))
        flat.append(sub * TOPK + b)
    cand.append(v1[SUBLANES:TOPK] + v2[0:1])
    flat.append((sub + SUBLANES) * TOPK)
    cand = jnp.concatenate(cand, axis=0)
    flat = jnp.concatenate(flat, axis=0)
    cand0 = cand
    taken = jnp.zeros(cand.shape, F32)
    for _ in range(TOPK):
        m = jnp.max(cand, axis=0, keepdims=True)
        first = jnp.min(jnp.where(cand == m, flat, TOPK * TOPK), axis=0, keepdims=True)
        hit = flat == first
        taken = jnp.where(hit, 1.0, taken)
        cand = jnp.where(hit, neg, cand)
    top = v1[0:1] + v2[0:1]
    z = jnp.sum(jnp.where(taken > 0.0, jnp.exp(cand0 - top), 0.0), axis=0, keepdims=True)
    lo = taken[40:48] + taken[48:56] + taken[56:64]
    for a, r0, r1 in ((0, 0, 16), (1, 16, 24), (2, 24, 32), (3, 32, 40)):
        lo = jnp.where(sub == a, jnp.sum(taken[r0:r1], axis=0, keepdims=True), lo)
    count = jnp.concatenate([lo, taken[64:72]], axis=0)
    return count, z


def _peer_route_body(h1_ref, wq_ref, k1_ref, k2_ref, r2_ref, e2_ref, bt_ref, e1_ref, s_s,
                     *, heads, tb):
    half = k1_ref.shape[2]
    n_lt = tb // LANES
    q = jnp.dot(h1_ref[...].astype(BF16), wq_ref[...], preferred_element_type=F32).astype(BF16)
    for h in range(heads):
        for side, k_ref in enumerate((k1_ref, k2_ref)):
            c0 = (2 * h + side) * half
            s = lax.dot_general(k_ref[h], q[:, c0:c0 + half], NT_DIMS, preferred_element_type=F32)
            for lt in range(n_lt):
                s_s[2 * h + side, lt] = s[:, lt * LANES:(lt + 1) * LANES]

    def per_tile(it, carry):
        h = it // n_lt
        lt = it % n_lt
        s1 = s_s[2 * h, lt]
        s2 = s_s[2 * h + 1, lt]
        v1, rank1 = _top_ranks(s1)
        v2, rank2 = _top_ranks(s2)
        count, z = _pair_counts(v1, v2)
        bt = jnp.zeros(s1.shape, F32)
        for a in range(TOPK):
            bt = jnp.where(rank1 == float(a), count[a:a + 1], bt)
        r2_ref[h, lt] = rank2
        e2_ref[h, lt] = jnp.exp(s2 - v2[0:1])
        bt_ref[h, lt] = bt
        e1_ref[h, lt] = jnp.exp(s1 - v1[0:1]) / z
        return carry

    lax.fori_loop(0, heads * n_lt, per_tile, 0)


def _peer_route(h1, wq, keys1, keys2, *, tb):
    n, d = h1.shape
    heads, n_keys, half = keys1.shape
    assert n_keys == N_KEYS
    n_lt = tb // LANES
    tab = jax.ShapeDtypeStruct((heads, n // LANES, N_KEYS, LANES), F32)
    tab_spec = pl.BlockSpec((heads, n_lt, N_KEYS, LANES), lambda i: (0, i, 0, 0))
    body = functools.partial(_peer_route_body, heads=heads, tb=tb)
    return pl.pallas_call(
        body,
        grid=(n // tb,),
        in_specs=[
            pl.BlockSpec((tb, d), lambda i: (i, 0)),
            pl.BlockSpec(wq.shape, lambda i: (0, 0)),
            pl.BlockSpec(keys1.shape, lambda i: (0, 0, 0)),
            pl.BlockSpec(keys2.shape, lambda i: (0, 0, 0)),
        ],
        out_specs=[tab_spec] * 4,
        out_shape=[tab] * 4,
        scratch_shapes=[pltpu.VMEM((2 * heads, n_lt, N_KEYS, LANES), F32)],
        compiler_params=pltpu.CompilerParams(
            dimension_semantics=("arbitrary",), vmem_limit_bytes=VMEM_LIMIT),
        name="peer_route",
    )(h1, wq, keys1, keys2)


def _peer_dense_body(h1_ref, u_ref, vt_ref, r2_ref, e2_ref, bt_ref, e1_ref, g2_ref, b2_ref, out_ref,
                     hb_s, acc_s, a_s, c_s, *, heads, tb, et, alpha):
    e = pl.program_id(1)
    rows = et // N_KEYS
    n_lt = tb // LANES

    @pl.when(e == 0)
    def _():
        hb_s[...] = h1_ref[...].astype(BF16)
        acc_s[...] = jnp.zeros(acc_s.shape, F32)

    a_s[...] = lax.dot_general(u_ref[...], hb_s[...], NT_DIMS, preferred_element_type=F32)
    for r in range(rows):
        key1 = e * rows + r
        for lt in range(n_lt):
            lanes = slice(lt * LANES, (lt + 1) * LANES)
            gate = jnp.zeros((N_KEYS, LANES), F32)
            for h in range(heads):
                bt = bt_ref[h, lt, pl.ds(key1, 1), :]
                e1 = e1_ref[h, lt, pl.ds(key1, 1), :]
                gate = gate + jnp.where(r2_ref[h, lt] < bt, e2_ref[h, lt], 0.0) * e1
            act = _gelu_erf(a_s[r * N_KEYS:(r + 1) * N_KEYS, lanes])
            c_s[r * N_KEYS:(r + 1) * N_KEYS, lanes] = (gate * act).astype(BF16)
    acc_s[...] += jnp.dot(vt_ref[...], c_s[...], preferred_element_type=F32)

    @pl.when(e == pl.num_programs(1) - 1)
    def _():
        y = acc_s[...].T
        out_ref[...] = _layer_norm(alpha * h1_ref[...] + y, g2_ref[...], b2_ref[...])


def _peer_dense(h1, u, vt, r2, e2, bt, e1, g2, b2, *, tb, et, alpha):
    n, d = h1.shape
    n_exp = u.shape[0]
    heads = r2.shape[0]
    n_lt = tb // LANES
    tab_spec = pl.BlockSpec((heads, n_lt, N_KEYS, LANES), lambda i, e: (0, i, 0, 0))
    body = functools.partial(_peer_dense_body, heads=heads, tb=tb, et=et, alpha=alpha)
    return pl.pallas_call(
        body,
        grid=(n // tb, n_exp // et),
        in_specs=[
            pl.BlockSpec((tb, d), lambda i, e: (i, 0)),
            pl.BlockSpec((et, d), lambda i, e: (e, 0)),
            pl.BlockSpec((d, et), lambda i, e: (0, e)),
            tab_spec, tab_spec, tab_spec, tab_spec,
            pl.BlockSpec((1, d), lambda i, e: (0, 0)),
            pl.BlockSpec((1, d), lambda i, e: (0, 0)),
        ],
        out_specs=pl.BlockSpec((tb, d), lambda i, e: (i, 0)),
        out_shape=jax.ShapeDtypeStruct((n, d), F32),
        scratch_shapes=[
            pltpu.VMEM((tb, d), BF16),
            pltpu.VMEM((d, tb), F32),
            pltpu.VMEM((et, tb), F32),
            pltpu.VMEM((et, tb), BF16),
        ],
        compiler_params=pltpu.CompilerParams(
            dimension_semantics=("arbitrary", "arbitrary"), vmem_limit_bytes=VMEM_LIMIT),
        name="peer_dense",
    )(h1, u, vt, r2, e2, bt, e1, g2, b2)


def _block_sizes(seq):
    tb = min(512, seq)
    return tb, 512


def _forward(x, meta_tokens, ln_in_g, ln_in_b, w_in, pool_w, pool_scale, pool_proj, conv_w, conv_b,
             conv_proj, w_out, ln1_g, ln1_b, peer_wq, peer_keys_1, peer_keys_2, expert_u, expert_v,
             ln2_g, ln2_b, blocks=None):
    bsz, seq, d = x.shape
    depth = w_in.shape[0]
    assert depth == 1 and meta_tokens.shape[0] == N_META
    alpha = (2.0 * depth) ** 0.25
    tb, et = blocks or _block_sizes(seq)
    assert seq % tb == 0 and tb % LANES == 0 and expert_u.shape[1] % et == 0
    row = lambda v: v.reshape(1, -1).astype(F32)
    x2 = x.reshape(bsz * seq, d)

    pooled, ybp, gates = _mixer_in(
        x2, meta_tokens, row(ln_in_g), row(ln_in_b), w_in[0].astype(BF16),
        conv_w[0, :, 0, :], row(conv_b[0]), tb=tb, seq=seq)
    h1 = _mixer_out(
        x2, row(ln_in_g), row(ln_in_b), pooled, ybp, gates, pool_w[0].astype(BF16),
        row(pool_scale[0]), pool_proj[0].astype(BF16), conv_proj[0].astype(BF16),
        w_out[0].astype(BF16), row(ln1_g[0]), row(ln1_b[0]), tb=tb, alpha=alpha)
    r2, e2, bt, e1 = _peer_route(
        h1, peer_wq[0].astype(BF16), peer_keys_1[0].astype(BF16), peer_keys_2[0].astype(BF16), tb=tb)
    out = _peer_dense(
        h1, expert_u[0].astype(BF16), expert_v[0].astype(BF16).T, r2, e2, bt, e1,
        row(ln2_g[0]), row(ln2_b[0]), tb=tb, et=et, alpha=alpha)
    return out.reshape(bsz, seq, d)


def kernel(x, meta_tokens, ln_in_g, ln_in_b, w_in, pool_w, pool_scale, pool_proj, conv_w, conv_b,
           conv_proj, w_out, ln1_g, ln1_b, peer_wq, peer_keys_1, peer_keys_2, expert_u, expert_v,
           ln2_g, ln2_b):
    return _forward(x, meta_tokens, ln_in_g, ln_in_b, w_in, pool_w, pool_scale, pool_proj, conv_w,
                    conv_b, conv_proj, w_out, ln1_g, ln1_b, peer_wq, peer_keys_1, peer_keys_2,
                    expert_u, expert_v, ln2_g, ln2_b)
```

```python
import functools

import jax
import jax.numpy as jnp
from jax import lax
from jax.experimental import pallas as pl
from jax.experimental.pallas import tpu as pltpu

F32 = jnp.float32
BF16 = jnp.bfloat16

LN_EPS = 1e-5
N_META = 16
POOL_WINDOWS = (2, 4, 8, 16)
CONV_K = 3
N_KEYS = 128
TOPK = 16
RANK_NONE = float(TOPK)
LANES = 128
SUBLANES = 8
WIDE = 2 * LANES
VMEM_LIMIT = 56 * 1024 * 1024

NT_DIMS = (((1,), (1,)), ((), ()))


def _layer_norm(x, g, b):
    mu = jnp.mean(x, axis=-1, keepdims=True)
    xc = x - mu
    var = jnp.mean(xc * xc, axis=-1, keepdims=True)
    return xc * lax.rsqrt(var + LN_EPS) * g + b


def _mixer_in_body(x_ref, meta_ref, g_ref, b_ref, w_ref, cw_ref, cb_ref,
                   pooled_ref, ybp_ref, gates_ref,
                   hn_s, zh_s, zb_s, atail_s, ptail_s, *, tb, blocks_per_seq, gc):
    i = pl.program_id(0)
    j = pl.program_id(1)
    seq_start = (i % blocks_per_seq) == 0

    @pl.when(j == 0)
    def _():
        hn_s[0:tb, :] = _layer_norm(x_ref[...], g_ref[...], b_ref[...]).astype(BF16)
        hn_s[tb:tb + N_META, :] = _layer_norm(meta_ref[...], g_ref[...], b_ref[...]).astype(BF16)

    chunk = jnp.dot(hn_s[...], w_ref[...], preferred_element_type=F32)

    def with_halo(cur, tail_s):
        @pl.when(seq_start)
        def _():
            tail_s[...] = cur[tb:tb + N_META]
        full = jnp.concatenate([cur[:tb], tail_s[...]], axis=0)
        tail_s[...] = cur[tb - N_META:tb]
        return full

    @pl.when(j == 0)
    def _():
        full = with_halo(chunk, atail_s)
        for g, w in enumerate(POOL_WINDOWS):
            s = full[:, g * gc:(g + 1) * gc]
            a_g = s[:tb]
            k = 1
            while k < w:
                s = s + pltpu.roll(s, k, axis=0)
                k *= 2
            pooled_ref[:, g * gc:(g + 1) * gc] = (s[:tb] * (1.0 / w) - a_g).astype(BF16)

    @pl.when(j == 1)
    def _():
        zh_s[...] = chunk

    @pl.when(j == 2)
    def _():
        zb_s[...] = chunk[:tb]

    @pl.when(j == 3)
    def _():
        full = with_halo(chunk * zh_s[...], ptail_s)
        conv = (cw_ref[2:3, :] * full + cw_ref[1:2, :] * pltpu.roll(full, 1, axis=0)
                + cw_ref[0:1, :] * pltpu.roll(full, 2, axis=0) + cb_ref[...])
        ybp_ref[...] = (zb_s[...] * conv[:tb]).astype(BF16)

    @pl.when(j >= 4)
    def _():
        gates_ref[...] = jax.nn.sigmoid(chunk[:tb]).astype(BF16)


def _mixer_in(x2, meta, g, b, w_in, conv_w, conv_b, *, tb, seq):
    n, d = x2.shape
    cw = conv_w.shape[1]
    n_chunks = w_in.shape[1] // cw
    gc = cw // len(POOL_WINDOWS)
    body = functools.partial(_mixer_in_body, tb=tb, blocks_per_seq=seq // tb, gc=gc)
    return pl.pallas_call(
        body,
        grid=(n // tb, n_chunks),
        in_specs=[
            pl.BlockSpec((tb, d), lambda i, j: (i, 0)),
            pl.BlockSpec((N_META, d), lambda i, j: (0, 0)),
            pl.BlockSpec((1, d), lambda i, j: (0, 0)),
            pl.BlockSpec((1, d), lambda i, j: (0, 0)),
            pl.BlockSpec((d, cw), lambda i, j: (0, j)),
            pl.BlockSpec((CONV_K, cw), lambda i, j: (0, 0)),
            pl.BlockSpec((1, cw), lambda i, j: (0, 0)),
        ],
        out_specs=[
            pl.BlockSpec((tb, cw), lambda i, j: (i, 0)),
            pl.BlockSpec((tb, cw), lambda i, j: (i, 0)),
            pl.BlockSpec((tb, cw), lambda i, j: (i, jnp.maximum(j - 4, 0))),
        ],
        out_shape=[
            jax.ShapeDtypeStruct((n, cw), BF16),
            jax.ShapeDtypeStruct((n, cw), BF16),
            jax.ShapeDtypeStruct((n, (n_chunks - 4) * cw), BF16),
        ],
        scratch_shapes=[
            pltpu.VMEM((tb + N_META, d), BF16),
            pltpu.VMEM((tb + N_META, cw), F32),
            pltpu.VMEM((tb, cw), F32),
            pltpu.VMEM((N_META, cw), F32),
            pltpu.VMEM((N_META, cw), F32),
        ],
        compiler_params=pltpu.CompilerParams(
            dimension_semantics=("arbitrary", "arbitrary"), vmem_limit_bytes=VMEM_LIMIT),
        name="mixer_in",
    )(x2, meta, g, b, w_in, conv_w, conv_b)


def _mixer_out_body(x_ref, gin_ref, bin_ref, pooled_ref, ybp_ref, gates_ref, poolw_ref, pscale_ref,
                    pproj_ref, cproj_ref, wout_ref, g1_ref, b1_ref, h1_ref, h1t_ref, *, alpha, gc):
    d = x_ref.shape[1]
    hn = _layer_norm(x_ref[...], gin_ref[...], bin_ref[...])
    pooled = pooled_ref[...]
    parts = [jnp.dot(pooled[:, g * gc:(g + 1) * gc], poolw_ref[g], preferred_element_type=F32)
             for g in range(len(POOL_WINDOWS))]
    ya = (jnp.concatenate(parts, axis=1) * pscale_ref[...]).astype(BF16)
    ya = jnp.dot(ya, pproj_ref[...], preferred_element_type=F32)
    yb = jnp.dot(ybp_ref[...], cproj_ref[...], preferred_element_type=F32)
    merged = gates_ref[:, 0:d].astype(F32) * ya + gates_ref[:, d:2 * d].astype(F32) * yb
    mixed = jnp.dot(merged.astype(BF16), wout_ref[...], preferred_element_type=F32)
    h1 = _layer_norm(alpha * hn + mixed, g1_ref[...], b1_ref[...])
    h1_ref[...] = h1
    h1t_ref[...] = h1.T.astype(BF16)


def _mixer_out(x2, gin, bin_, pooled, ybp, gates, pool_w, pscale, pproj, cproj, wout, g1, b1,
               *, tb, alpha):
    n, d = x2.shape
    cw = pooled.shape[1]
    gc = cw // len(POOL_WINDOWS)
    const2 = lambda i: (0, 0)
    body = functools.partial(_mixer_out_body, alpha=alpha, gc=gc)
    return pl.pallas_call(
        body,
        grid=(n // tb,),
        in_specs=[
            pl.BlockSpec((tb, d), lambda i: (i, 0)),
            pl.BlockSpec((1, d), const2),
            pl.BlockSpec((1, d), const2),
            pl.BlockSpec((tb, cw), lambda i: (i, 0)),
            pl.BlockSpec((tb, cw), lambda i: (i, 0)),
            pl.BlockSpec((tb, 2 * d), lambda i: (i, 0)),
            pl.BlockSpec(pool_w.shape, lambda i: (0, 0, 0)),
            pl.BlockSpec((1, cw), const2),
            pl.BlockSpec((cw, d), const2, pipeline_mode=pl.Buffered(1)),
            pl.BlockSpec((cw, d), const2, pipeline_mode=pl.Buffered(1)),
            pl.BlockSpec((d, d), const2, pipeline_mode=pl.Buffered(1)),
            pl.BlockSpec((1, d), const2),
            pl.BlockSpec((1, d), const2),
        ],
        out_specs=[pl.BlockSpec((tb, d), lambda i: (i, 0)), pl.BlockSpec((d, tb), lambda i: (0, i))],
        out_shape=[jax.ShapeDtypeStruct((n, d), F32), jax.ShapeDtypeStruct((d, n), BF16)],
        compiler_params=pltpu.CompilerParams(
            dimension_semantics=("arbitrary",), vmem_limit_bytes=VMEM_LIMIT),
        name="mixer_out",
    )(x2, gin, bin_, pooled, ybp, gates, pool_w, pscale, pproj, cproj, wout, g1, b1)


def _extract_top(s, index, sentinel, tie_break):
    order = jnp.full(s.shape, RANK_NONE, F32)
    vals = []
    for a in range(TOPK):
        m = jnp.max(s, axis=0, keepdims=True)
        hit = s == m
        if tie_break:
            first = jnp.min(jnp.where(hit, index, sentinel), axis=0, keepdims=True)
            hit = index == first
        order = jnp.where(hit, float(a), order)
        s = jnp.where(hit, -jnp.inf, s)
        vals.append(m)
    return jnp.concatenate(vals, axis=0), order


def _pair_candidates(v1, v2):
    sub = lax.broadcasted_iota(jnp.int32, (SUBLANES, LANES), 0)
    sub16 = lax.broadcasted_iota(jnp.int32, (TOPK, LANES), 0)
    cand, flat = [v1[0:1] + v2], [sub16]
    for a in (1, 2, 3):
        cand.append(v1[a:a + 1] + v2[0:SUBLANES])
        flat.append(a * TOPK + sub)
    for b in (0, 1, 2):
        cand.append(jnp.where(sub >= 4, v1[0:SUBLANES] + v2[b:b + 1], -jnp.inf))
        flat.append(sub * TOPK + b)
    cand.append(v1[SUBLANES:TOPK] + v2[0:1])
    flat.append((sub + SUBLANES) * TOPK)
    return jnp.concatenate(cand, axis=0), jnp.concatenate(flat, axis=0)


def _pair_counts(taken):
    sub = lax.broadcasted_iota(jnp.int32, (SUBLANES, LANES), 0)
    lo = taken[40:48] + taken[48:56] + taken[56:64]
    for a, r0, r1 in ((0, 0, 16), (1, 16, 24), (2, 24, 32), (3, 32, 40)):
        lo = jnp.where(sub == a, jnp.sum(taken[r0:r1], axis=0, keepdims=True), lo)
    return jnp.concatenate([lo, taken[64:72]], axis=0)


def _route_tile(s1, s2, tie_break):
    key = lax.broadcasted_iota(jnp.int32, s1.shape, 0)
    v1, rank1 = _extract_top(s1, key, N_KEYS, tie_break)
    v2, rank2 = _extract_top(s2, key, N_KEYS, tie_break)
    cand, flat = _pair_candidates(v1, v2)
    _, order = _extract_top(cand, flat, TOPK * TOPK, tie_break)
    one, zero = jnp.float32(1.0), jnp.float32(0.0)
    taken = jnp.where(order < RANK_NONE, one, zero)
    count = _pair_counts(taken)
    z = jnp.sum(taken * jnp.exp(cand - (v1[0:1] + v2[0:1])), axis=0, keepdims=True)
    bt = jnp.zeros(s1.shape, F32)
    for a in range(TOPK):
        bt = jnp.where(rank1 == float(a), count[a:a + 1], bt)
    removed = lambda o: jnp.sum(jnp.where(o < RANK_NONE, one, zero), axis=0, keepdims=True)
    clean = (jnp.abs(removed(rank1) - TOPK) + jnp.abs(removed(rank2) - TOPK)
             + jnp.abs(jnp.sum(taken, axis=0, keepdims=True) - TOPK))
    return rank2, jnp.exp(s2 - v2[0:1]), bt, jnp.exp(s1 - v1[0:1]) / z, clean


def _peer_route_body(h1t_ref, wqt_ref, k1_ref, k2_ref, r2_ref, e2_ref, bt_ref, e1_ref, s_s,
                     *, heads, tb):
    half = k1_ref.shape[2]
    n_lt = tb // LANES
    qt = jnp.dot(wqt_ref[...], h1t_ref[...], preferred_element_type=F32).astype(BF16)
    for h in range(heads):
        for side, k_ref in enumerate((k1_ref, k2_ref)):
            c0 = (2 * h + side) * half
            s = jnp.dot(k_ref[h], qt[c0:c0 + half, :], preferred_element_type=F32)
            for lt in range(n_lt):
                s_s[2 * h + side, lt] = s[:, lt * LANES:(lt + 1) * LANES]

    n_wt = tb // WIDE

    def per_tile(it, carry):
        h = it // n_wt
        wt = it % n_wt
        def tables(sub, tie_break):
            lt = wt * (WIDE // LANES) + sub
            lanes = slice(sub * LANES, (sub + 1) * LANES)
            rank2, e2, bt, e1, clean = _route_tile(s_s[2 * h, lt], s_s[2 * h + 1, lt], tie_break)
            r2_ref[h, wt, :, lanes] = rank2.astype(BF16)
            e2_ref[h, wt, :, lanes] = e2.astype(BF16)
            bt_ref[h, wt, :, lanes] = bt
            e1_ref[h, wt, :, lanes] = e1
            return jnp.max(clean) > 0.0

        subs = range(WIDE // LANES)
        redo = [tables(sub, tie_break=False) for sub in subs]
        for sub in subs:
            @pl.when(redo[sub])
            def _():
                tables(sub, tie_break=True)
        return carry

    lax.fori_loop(0, heads * n_wt, per_tile, 0)


def _peer_route(h1t, wqt, keys1, keys2, *, tb):
    d, n = h1t.shape
    heads, n_keys, half = keys1.shape
    assert n_keys == N_KEYS
    n_lt = tb // LANES
    tab_shape = (heads, n // WIDE, N_KEYS, WIDE)
    tab_spec = pl.BlockSpec((heads, tb // WIDE, N_KEYS, WIDE), lambda i: (0, i, 0, 0))
    body = functools.partial(_peer_route_body, heads=heads, tb=tb)
    return pl.pallas_call(
        body,
        grid=(n // tb,),
        in_specs=[
            pl.BlockSpec((d, tb), lambda i: (0, i)),
            pl.BlockSpec(wqt.shape, lambda i: (0, 0)),
            pl.BlockSpec(keys1.shape, lambda i: (0, 0, 0)),
            pl.BlockSpec(keys2.shape, lambda i: (0, 0, 0)),
        ],
        out_specs=[tab_spec] * 4,
        out_shape=[jax.ShapeDtypeStruct(tab_shape, dt) for dt in (BF16, BF16, F32, F32)],
        scratch_shapes=[pltpu.VMEM((2 * heads, n_lt, N_KEYS, LANES), F32)],
        compiler_params=pltpu.CompilerParams(
            dimension_semantics=("arbitrary",), vmem_limit_bytes=VMEM_LIMIT),
        name="peer_route",
    )(h1t, wqt, keys1, keys2)


_ERFC_P = 0.3275911
_ERFC_A = (0.254829592, -0.284496736, 1.421413741, -1.453152027, 1.061405429)
_LOG2E = 1.4426950408889634


def _gelu(x):
    a1, a2, a3, a4, a5 = (0.5 * a for a in _ERFC_A)
    t = 1.0 / (1.0 + (_ERFC_P * 0.5 ** 0.5) * jnp.abs(x))
    poly = t * (a1 + t * (a2 + t * (a3 + t * (a4 + t * a5))))
    q = poly * jnp.exp2(x * x * (-0.5 * _LOG2E))
    return x * jnp.where(x >= 0.0, 1.0 - q, q)


def _zero_after(token):
    bits = pltpu.bitcast(token, jnp.uint32)
    bits = lax.shift_right_logical(lax.shift_right_logical(bits, jnp.uint32(16)), jnp.uint32(16))
    return pltpu.bitcast(bits, F32)


def _gated_unit(a_ref, c_ref, key1, r, wt, token, *, r2_ref, e2_ref, bt_ref, e1_ref, heads):
    blk = (SUBLANES, WIDE)
    zero = jnp.zeros(blk, BF16)
    z = _zero_after(token)
    z = jnp.concatenate([z] * (WIDE // LANES), axis=1)
    lanes = slice(wt * WIDE, (wt + 1) * WIDE)
    bts = [jnp.broadcast_to(bt_ref[h, wt, key1:key1 + 1, :], blk).astype(BF16) for h in range(heads)]
    e1s = [jnp.broadcast_to(e1_ref[h, wt, key1:key1 + 1, :], blk).astype(BF16) for h in range(heads)]
    for sb in range(N_KEYS // SUBLANES):
        keys = slice(sb * SUBLANES, (sb + 1) * SUBLANES)
        rows = slice(r * N_KEYS + sb * SUBLANES, r * N_KEYS + (sb + 1) * SUBLANES)
        gate = zero
        for h in range(heads):
            gate = gate + jnp.where(r2_ref[h, wt, keys, :] < bts[h], e2_ref[h, wt, keys, :], zero) * e1s[h]
        c_ref[rows, lanes] = gate * _gelu(a_ref[rows, lanes] + z).astype(BF16)


def _peer_dense_body(ht_ref, h1_ref, ufirst_ref, uodd_ref, unext_ref, vprev_ref, vcur_ref,
                     r2_ref, e2_ref, bt_ref, e1_ref, g2_ref, b2_ref, out_ref,
                     acc_s, a0_s, a1_s, c0_s, c1_s, *, heads, tb, alpha):
    e = pl.program_id(1)
    n_steps = pl.num_programs(1) - 1
    rows = a0_s.shape[0] // N_KEYS
    half = rows // 2
    live = e < n_steps
    gated = functools.partial(_gated_unit, r2_ref=r2_ref, e2_ref=e2_ref, bt_ref=bt_ref,
                              e1_ref=e1_ref, heads=heads)
    dot = functools.partial(jnp.dot, preferred_element_type=F32)
    d = acc_s.shape[0]
    n_wt = tb // WIDE

    def down_pieces(v_ref, c_ref, n=4):
        def piece(m):
            rs = slice(m * (d // n), (m + 1) * (d // n))
            res = dot(v_ref[rs, :], c_ref[...])
            acc_s[rs, :] += res
            return res[0:SUBLANES, 0:LANES]
        return [functools.partial(piece, m) for m in range(n)]

    def up_pieces(u_ref, a_ref, n=2):
        et = a_ref.shape[0]

        def piece(m, w):
            rs = slice(m * (et // n), (m + 1) * (et // n))
            ls = slice(w * WIDE, (w + 1) * WIDE)
            res = dot(u_ref[rs, :], ht_ref[:, ls])
            a_ref[rs, ls] = res
            return res[0:SUBLANES, 0:LANES]
        return [functools.partial(piece, m, w) for w in range(n_wt) for m in range(n)]

    def gate_units(a_ref, c_ref, key_row0, r_lo, r_hi):
        return [functools.partial(gated, a_ref, c_ref, key_row0 + r, r, wt)
                for r in range(r_lo, r_hi) for wt in range(n_wt)]

    def chain(pieces, units):
        n_p, n_u = len(pieces), len(units)
        for k, piece in enumerate(pieces):
            token = piece()
            for unit in units[-(-k * n_u // n_p):-(-(k + 1) * n_u // n_p)]:
                unit(token)

    @pl.when(e == 0)
    def _():
        acc_s[...] = jnp.zeros(acc_s.shape, F32)
        c1_s[...] = jnp.zeros(c1_s.shape, BF16)
        a0_s[...] = dot(ufirst_ref[...], ht_ref[...])

    @pl.when(live)
    def _():
        chain(down_pieces(vprev_ref, c1_s), gate_units(a0_s, c0_s, 0, 0, half))
        chain(up_pieces(uodd_ref, a1_s), gate_units(a0_s, c0_s, 0, half, rows))
        chain(down_pieces(vcur_ref, c0_s), gate_units(a1_s, c1_s, rows, 0, half))
        chain(up_pieces(unext_ref, a0_s), gate_units(a1_s, c1_s, rows, half, rows))

    @pl.when(e == n_steps)
    def _():
        y = (acc_s[...] + dot(vprev_ref[...], c1_s[...])).T
        out_ref[...] = _layer_norm(alpha * h1_ref[...] + y, g2_ref[...], b2_ref[...])


def _peer_dense(h1t, h1, u, vt, r2, e2, bt, e1, g2, b2, *, tb, et, alpha):
    n, d = h1.shape
    n_tiles = u.shape[0] // et
    n_steps = n_tiles // 2
    heads = r2.shape[0]
    tab_spec = pl.BlockSpec((heads, tb // WIDE, N_KEYS, WIDE), lambda i, e: (0, i, 0, 0))
    step_rows = 2 * et // N_KEYS
    assert step_rows % SUBLANES == 0
    row_spec = pl.BlockSpec((heads, tb // WIDE, step_rows, WIDE),
                            lambda i, e: (0, i, jnp.minimum(e, n_steps - 1), 0))
    once = pl.Buffered(1)
    body = functools.partial(_peer_dense_body, heads=heads, tb=tb, alpha=alpha)
    return pl.pallas_call(
        body,
        grid=(n // tb, n_steps + 1),
        in_specs=[
            pl.BlockSpec((d, tb), lambda i, e: (0, i)),
            pl.BlockSpec((tb, d), lambda i, e: (i, 0), pipeline_mode=once),
            pl.BlockSpec((et, d), lambda i, e: (0, 0), pipeline_mode=once),
            pl.BlockSpec((et, d), lambda i, e: (jnp.minimum(2 * e + 1, n_tiles - 1), 0)),
            pl.BlockSpec((et, d), lambda i, e: (jnp.minimum(2 * e + 2, n_tiles - 2), 0)),
            pl.BlockSpec((d, et), lambda i, e: (0, jnp.maximum(2 * e - 1, 0))),
            pl.BlockSpec((d, et), lambda i, e: (0, jnp.minimum(2 * e, n_tiles - 2))),
            tab_spec, tab_spec, row_spec, row_spec,
            pl.BlockSpec((1, d), lambda i, e: (0, 0)),
            pl.BlockSpec((1, d), lambda i, e: (0, 0)),
        ],
        out_specs=pl.BlockSpec((tb, d), lambda i, e: (i, 0), pipeline_mode=once),
        out_shape=jax.ShapeDtypeStruct((n, d), F32),
        scratch_shapes=[
            pltpu.VMEM((d, tb), F32),
            pltpu.VMEM((et, tb), F32),
            pltpu.VMEM((et, tb), F32),
            pltpu.VMEM((et, tb), BF16),
            pltpu.VMEM((et, tb), BF16),
        ],
        compiler_params=pltpu.CompilerParams(
            dimension_semantics=("arbitrary", "arbitrary"), vmem_limit_bytes=VMEM_LIMIT),
        name="peer_dense",
    )(h1t, h1, u, u, u, vt, vt, r2, e2, bt, e1, g2, b2)


def _block_sizes(seq):
    tb = min(512, seq)
    return tb, 512


def _forward(x, meta_tokens, ln_in_g, ln_in_b, w_in, pool_w, pool_scale, pool_proj, conv_w, conv_b,
             conv_proj, w_out, ln1_g, ln1_b, peer_wq, peer_keys_1, peer_keys_2, expert_u, expert_v,
             ln2_g, ln2_b, blocks=None):
    bsz, seq, d = x.shape
    depth = w_in.shape[0]
    assert depth == 1 and meta_tokens.shape[0] == N_META
    alpha = (2.0 * depth) ** 0.25
    tb, et = blocks or _block_sizes(seq)
    assert seq % tb == 0 and tb % LANES == 0 and expert_u.shape[1] % (2 * et) == 0
    row = lambda v: v.reshape(1, -1).astype(F32)
    x2 = x.reshape(bsz * seq, d)

    pooled, ybp, gates = _mixer_in(
        x2, meta_tokens, row(ln_in_g), row(ln_in_b), w_in[0].astype(BF16),
        conv_w[0, :, 0, :], row(conv_b[0]), tb=tb, seq=seq)
    h1, h1t = _mixer_out(
        x2, row(ln_in_g), row(ln_in_b), pooled, ybp, gates, pool_w[0].astype(BF16),
        row(pool_scale[0]), pool_proj[0].astype(BF16), conv_proj[0].astype(BF16),
        w_out[0].astype(BF16), row(ln1_g[0]), row(ln1_b[0]), tb=tb // 2, alpha=alpha)
    r2, e2, bt, e1 = _peer_route(
        h1t, peer_wq[0].astype(BF16).T, peer_keys_1[0].astype(BF16), peer_keys_2[0].astype(BF16), tb=tb)
    out = _peer_dense(
        h1t, h1, expert_u[0].astype(BF16), expert_v[0].astype(BF16).T, r2, e2, bt, e1,
        row(ln2_g[0]), row(ln2_b[0]), tb=tb, et=et, alpha=alpha)
    return out.reshape(bsz, seq, d)


def kernel(x, meta_tokens, ln_in_g, ln_in_b, w_in, pool_w, pool_scale, pool_proj, conv_w, conv_b,
           conv_proj, w_out, ln1_g, ln1_b, peer_wq, peer_keys_1, peer_keys_2, expert_u, expert_v,
           ln2_g, ln2_b):
    return _forward(x, meta_tokens, ln_in_g, ln_in_b, w_in, pool_w, pool_scale, pool_proj, conv_w,
                    conv_b, conv_proj, w_out, ln1_g, ln1_b, peer_wq, peer_keys_1, peer_keys_2,
                    expert_u, expert_v, ln2_g, ln2_b)
```
